```python
import math
import jax, jax.numpy as jnp
from jax import lax
import numpy as np

D_MODEL = 4096
BATCH = 2
SEQ = 8192
DEPTH = 4

N_A = DEPTH // 2
N_B = DEPTH - N_A
MAIN_WIDTH = 3 * D_MODEL // 4
LRU_BLOCK_W = 256
LRU_BLOCKS = MAIN_WIDTH // LRU_BLOCK_W
LRU_C = 8.0
CONV_W = 4
SB_HEAD_DIM = 128
SB_HEADS = MAIN_WIDTH // SB_HEAD_DIM
SB_BLOCK = 128
MEM_TOKENS = 256
MEM_HEADS = 4
MEM_HEAD_DIM = 256
MEM_WIDTH = MEM_HEADS * MEM_HEAD_DIM
IN_COLS = 2 * MAIN_WIDTH + 2 * MEM_WIDTH
MIX_WIDTH = MAIN_WIDTH + MEM_WIDTH
EPS = 1e-6

kernel_name = "yoco_rglru_stickbreaking_memory_trunk"


def rms_norm(x, g):
    xf = x.astype(jnp.float32)
    y = xf * lax.rsqrt(jnp.mean(xf * xf, axis=-1, keepdims=True) + EPS)
    return (y * g.astype(jnp.float32)).astype(x.dtype)


def causal_depthwise_conv(u, w, b):
    S = u.shape[1]
    up = jnp.pad(u, ((0, 0), (CONV_W - 1, 0), (0, 0)))
    out = b
    for k in range(CONV_W):
        out = out + up[:, k:k + S] * w[k]
    return out


def rg_lru(u, gate_a_w, gate_a_b, gate_x_w, gate_x_b, lam):
    B, S, W = u.shape
    ub = u.reshape(B, S, LRU_BLOCKS, LRU_BLOCK_W)
    r = jax.nn.sigmoid(jnp.einsum('bsni,nij->bsnj', ub, gate_a_w).reshape(B, S, W) + gate_a_b)
    i = jax.nn.sigmoid(jnp.einsum('bsni,nij->bsnj', ub, gate_x_w).reshape(B, S, W) + gate_x_b)
    log_a = -LRU_C * r.astype(jnp.float32) * jax.nn.softplus(-lam.astype(jnp.float32))
    a = jnp.exp(log_a)
    b = jnp.sqrt(-jnp.expm1(2.0 * log_a)) * (i * u).astype(jnp.float32)

    def combine(left, right):
        a1, b1 = left
        a2, b2 = right
        return a1 * a2, a2 * b1 + b2

    _, h = lax.associative_scan(combine, (a, b), axis=1)
    return h.astype(u.dtype)


def stick_breaking_attention(q, k, v):
    S = q.shape[1]
    scale = 1.0 / math.sqrt(q.shape[-1])
    outs = []
    for start in range(0, S, SB_BLOCK):
        end = start + SB_BLOCK
        qb = q[:, start:end]
        kb = k[:, :end]
        vb = v[:, :end]
        z = jnp.einsum('bqhd,bkhd->bhqk', qb, kb).astype(jnp.float32) * scale
        t_idx = start + jnp.arange(SB_BLOCK)[:, None]
        s_idx = jnp.arange(end)[None, :]
        causal = s_idx < t_idx
        log_beta = jax.nn.log_sigmoid(z)
        log_1m_beta = jnp.where(causal, log_beta - z, 0.0)
        tail = lax.cumsum(log_1m_beta, axis=3, reverse=True) - log_1m_beta
        w = jnp.where(causal, jnp.exp(log_beta + tail), 0.0)
        outs.append(jnp.einsum('bhqk,bkhd->bqhd', w.astype(vb.dtype), vb))
    return jnp.concatenate(outs, axis=1)


def memory_cross_attention(mq, mem_kv):
    B, S, _ = mq.shape
    M = mem_kv.shape[1]
    q = mq.reshape(B, S, MEM_HEADS, MEM_HEAD_DIM)
    mk, mv = jnp.split(mem_kv, 2, axis=-1)
    mk = mk.reshape(B, M, MEM_HEADS, MEM_HEAD_DIM)
    mv = mv.reshape(B, M, MEM_HEADS, MEM_HEAD_DIM)
    s = jnp.einsum('bshd,bmhd->bhsm', q, mk).astype(jnp.float32) / math.sqrt(MEM_HEAD_DIM)
    p = jax.nn.softmax(s, axis=-1).astype(mv.dtype)
    return jnp.einsum('bhsm,bmhd->bshd', p, mv).reshape(B, S, MEM_WIDTH)


def setup_inputs(seed: int = 0) -> dict:
    key = jax.random.key(seed)
    ks = jax.random.split(key, 20)
    f32 = jnp.float32
    x = jax.random.normal(ks[0], (BATCH, SEQ, D_MODEL), f32)
    mem = jax.random.normal(ks[1], (BATCH, MEM_TOKENS, D_MODEL), f32)
    pre_gain = 1.0 + 0.05 * jax.random.normal(ks[2], (DEPTH, D_MODEL), f32)
    post_gain = 1.0 + 0.05 * jax.random.normal(ks[3], (DEPTH, D_MODEL), f32)
    w_in = jax.random.normal(ks[4], (DEPTH, D_MODEL, IN_COLS), f32) * D_MODEL ** -0.5
    w_out = jax.random.normal(ks[5], (DEPTH, MIX_WIDTH, D_MODEL), f32) * MIX_WIDTH ** -0.5
    w_mem_kv = jax.random.normal(ks[6], (DEPTH, D_MODEL, 2 * MEM_WIDTH), f32) * D_MODEL ** -0.5
    mem_norm_gain = 1.0 + 0.05 * jax.random.normal(ks[7], (D_MODEL,), f32)
    conv_w = jax.random.normal(ks[8], (N_A, CONV_W, MAIN_WIDTH), f32) * CONV_W ** -0.5
    conv_b = 0.02 * jax.random.normal(ks[9], (N_A, MAIN_WIDTH), f32)
    gate_a_w = jax.random.normal(ks[10], (N_A, LRU_BLOCKS, LRU_BLOCK_W, LRU_BLOCK_W), f32) * LRU_BLOCK_W ** -0.5
    gate_a_b = 0.02 * jax.random.normal(ks[11], (N_A, MAIN_WIDTH), f32)
    gate_x_w = jax.random.normal(ks[12], (N_A, LRU_BLOCKS, LRU_BLOCK_W, LRU_BLOCK_W), f32) * LRU_BLOCK_W ** -0.5
    gate_x_b = 0.02 * jax.random.normal(ks[13], (N_A, MAIN_WIDTH), f32)
    a_c = jax.random.uniform(ks[14], (N_A, MAIN_WIDTH), f32, 0.9, 0.999)
    base = a_c ** (1.0 / LRU_C)
    lru_lambda = jnp.log(base) - jnp.log1p(-base)
    kv_norm_gain = 1.0 + 0.05 * jax.random.normal(ks[15], (D_MODEL,), f32)
    w_kv = jax.random.normal(ks[16], (D_MODEL, 2 * MAIN_WIDTH), f32) * D_MODEL ** -0.5
    return {"x": x, "mem": mem, "pre_gain": pre_gain, "post_gain": post_gain,
            "w_in": w_in, "w_out": w_out, "w_mem_kv": w_mem_kv, "mem_norm_gain": mem_norm_gain,
            "conv_w": conv_w, "conv_b": conv_b, "gate_a_w": gate_a_w, "gate_a_b": gate_a_b,
            "gate_x_w": gate_x_w, "gate_x_b": gate_x_b, "lru_lambda": lru_lambda,
            "kv_norm_gain": kv_norm_gain, "w_kv": w_kv}


def reference(x, mem, pre_gain, post_gain, w_in, w_out, w_mem_kv, mem_norm_gain,
              conv_w, conv_b, gate_a_w, gate_a_b, gate_x_w, gate_x_b, lru_lambda,
              kv_norm_gain, w_kv):
    B, S, _ = x.shape
    mem_n = rms_norm(mem, mem_norm_gain)
    h = x
    k_shared = None
    v_shared = None
    for layer in range(DEPTH):
        y = rms_norm(h, pre_gain[layer])
        proj = y @ w_in[layer]
        br = proj[..., :MAIN_WIDTH]
        br_gate = proj[..., MAIN_WIDTH:2 * MAIN_WIDTH]
        mq = proj[..., 2 * MAIN_WIDTH:2 * MAIN_WIDTH + MEM_WIDTH]
        m_gate = proj[..., 2 * MAIN_WIDTH + MEM_WIDTH:]
        if layer < N_A:
            u = causal_depthwise_conv(br, conv_w[layer], conv_b[layer])
            br_out = rg_lru(u, gate_a_w[layer], gate_a_b[layer], gate_x_w[layer],
                            gate_x_b[layer], lru_lambda[layer])
        else:
            q = br.reshape(B, S, SB_HEADS, SB_HEAD_DIM)
            br_out = stick_breaking_attention(q, k_shared, v_shared).reshape(B, S, MAIN_WIDTH)
        m_out = memory_cross_attention(mq, mem_n @ w_mem_kv[layer])
        mix = jnp.concatenate([br_out * jax.nn.silu(br_gate), m_out * jax.nn.silu(m_gate)], axis=-1)
        h = h + rms_norm(mix @ w_out[layer], post_gain[layer])
        if layer == N_A - 1:
            kv = rms_norm(h, kv_norm_gain) @ w_kv
            k_shared = kv[..., :MAIN_WIDTH].reshape(B, S, SB_HEADS, SB_HEAD_DIM)
            v_shared = kv[..., MAIN_WIDTH:].reshape(B, S, SB_HEADS, SB_HEAD_DIM)
    return h
```

```python
import functools
import math

import jax
import jax.numpy as jnp
from jax import lax
from jax.experimental import pallas as pl
from jax.experimental.pallas import tpu as pltpu

EPS = 1e-6
LRU_C = 8.0
SB_HEAD_DIM = 128
MEM_HEADS = 4

V7X_VMEM_BYTES = 64 * 1024 * 1024
VMEM_LIMIT_BYTES = V7X_VMEM_BYTES - 8 * 1024 * 1024
SUBLANES = 8
BF16_ROWS = 16
MXU_DIM = 256

EXP_F32_ZERO_BELOW = -110.0

_F32 = jnp.float32
_BF16 = jnp.bfloat16


def _pick(n, pref):
    if n <= pref:
        return n
    t = pref
    while n % t:
        t //= 2
    return t


def _params(sem):
    return pltpu.CompilerParams(dimension_semantics=sem, vmem_limit_bytes=VMEM_LIMIT_BYTES)


def _norm_matmul_kernel(x_ref, g_ref, w_ref, o_ref, y_ref):
    tm = x_ref.shape[0]

    @pl.when(pl.program_id(1) == 0)
    def _():
        g = g_ref[...]

        def body(c, carry):
            r = pl.multiple_of(c * BF16_ROWS, BF16_ROWS)
            x = x_ref[pl.ds(r, BF16_ROWS), :]
            ms = jnp.mean(x * x, axis=-1, keepdims=True)
            y_ref[pl.ds(r, BF16_ROWS), :] = (x * lax.rsqrt(ms + EPS) * g).astype(y_ref.dtype)
            return carry

        lax.fori_loop(0, tm // BF16_ROWS, body, 0)

    o_ref[...] = jnp.dot(y_ref[...], w_ref[...], preferred_element_type=_F32).astype(o_ref.dtype)


def _norm_matmul(x, g, w):
    T, K = x.shape
    N = w.shape[1]
    tm = _pick(T, 512)
    tn = _pick(N, 1024)
    return pl.pallas_call(
        _norm_matmul_kernel,
        grid=(T // tm, N // tn),
        in_specs=[
            pl.BlockSpec((tm, K), lambda i, j: (i, 0)),
            pl.BlockSpec((1, K), lambda i, j: (0, 0)),
            pl.BlockSpec((K, tn), lambda i, j: (0, j)),
        ],
        out_specs=pl.BlockSpec((tm, tn), lambda i, j: (i, j)),
        out_shape=jax.ShapeDtypeStruct((T, N), _BF16),
        scratch_shapes=[pltpu.VMEM((tm, K), _BF16)],
        compiler_params=_params(("parallel", "arbitrary")),
    )(x, g.reshape(1, K), w)


def _out_proj_kernel(a_ref, b_ref, w_ref, h_ref, g_ref, o_ref, *, nka, nk, tn):
    k = pl.program_id(1)
    tm, D = o_ref.shape

    @pl.when(k == 0)
    def _():
        o_ref[...] = jnp.zeros_like(o_ref)

    def accumulate(lhs_ref):
        lhs = lhs_ref[...]
        for n in range(D // tn):
            cols = slice(n * tn, (n + 1) * tn)
            o_ref[:, cols] += jnp.dot(lhs, w_ref[:, cols], preferred_element_type=_F32)

    @pl.when(k < nka)
    def _():
        accumulate(a_ref)

    @pl.when(k >= nka)
    def _():
        accumulate(b_ref)

    @pl.when(k == nk - 1)
    def _():
        g = g_ref[...]

        def body(c, carry):
            r = pl.multiple_of(c * SUBLANES, SUBLANES)
            o = o_ref[pl.ds(r, SUBLANES), :]
            ms = jnp.mean(o * o, axis=-1, keepdims=True)
            o_ref[pl.ds(r, SUBLANES), :] = h_ref[pl.ds(r, SUBLANES), :] + o * lax.rsqrt(ms + EPS) * g
            return carry

        lax.fori_loop(0, tm // SUBLANES, body, 0)


def _out_proj(a, b, w, h, g):
    T, Wa = a.shape
    Wb = b.shape[1]
    D = w.shape[1]
    tm = _pick(T, 512)
    tk = _pick(math.gcd(Wa, Wb), 512)
    nka, nkb = Wa // tk, Wb // tk
    nk = nka + nkb
    tn = _pick(D, 1024)
    return pl.pallas_call(
        functools.partial(_out_proj_kernel, nka=nka, nk=nk, tn=tn),
        grid=(T // tm, nk),
        in_specs=[
            pl.BlockSpec((tm, tk), lambda i, k: (i, jnp.minimum(k, nka - 1))),
            pl.BlockSpec((tm, tk), lambda i, k: (i, jnp.maximum(k - nka, 0))),
            pl.BlockSpec((tk, D), lambda i, k: (k, 0)),
            pl.BlockSpec((tm, D), lambda i, k: (i, 0)),
            pl.BlockSpec((1, D), lambda i, k: (0, 0)),
        ],
        out_specs=pl.BlockSpec((tm, D), lambda i, k: (i, 0)),
        out_shape=jax.ShapeDtypeStruct((T, D), _F32),
        compiler_params=_params(("parallel", "arbitrary")),
    )(a, b, w, h, g.reshape(1, D))


def _rglru_kernel(br_ref, gate_ref, cw_ref, cb_ref, wa_ref, ab_ref, wx_ref, xb_ref, lam_ref,
                  o_ref, xs_ref, a_ref, b_ref, h_ref, carry_ref, *, conv_w):
    ts, bw = br_ref.shape
    halo = SUBLANES
    t = pl.program_id(2)

    @pl.when(t == 0)
    def _():
        xs_ref[0:halo, :] = jnp.zeros((halo, bw), _F32)
        carry_ref[...] = jnp.zeros_like(carry_ref)

    x = br_ref[...].astype(_F32)
    xs_ref[halo:halo + ts, :] = x
    cw = cw_ref[...]
    u = cb_ref[...] + cw[conv_w - 1:conv_w, :] * x
    for k in range(conv_w - 1):
        shift = conv_w - 1 - k
        u = u + cw[k:k + 1, :] * xs_ref[halo - shift:halo - shift + ts, :]
    xs_ref[0:halo, :] = x[ts - halo:ts, :]

    ub = u.astype(_BF16)
    r = jax.nn.sigmoid(jnp.dot(ub, wa_ref[...], preferred_element_type=_F32) + ab_ref[...])
    i = jax.nn.sigmoid(jnp.dot(ub, wx_ref[...], preferred_element_type=_F32) + xb_ref[...])
    nl = -lam_ref[...]
    softplus_nl = jnp.maximum(nl, 0.0) + jnp.log1p(jnp.exp(-jnp.abs(nl)))
    log_a = (-LRU_C) * r * softplus_nl
    a = jnp.exp(log_a)
    b = jnp.sqrt(-jnp.tanh(log_a) * (1.0 + a * a)) * (i * u)

    rowmod = lax.broadcasted_iota(jnp.int32, (ts, bw), 0) & (SUBLANES - 1)
    for s in (1, 2, 4):
        a_prev = pltpu.roll(a, s, axis=0)
        b_prev = pltpu.roll(b, s, axis=0)
        m = rowmod >= s
        b = jnp.where(m, a * b_prev + b, b)
        a = jnp.where(m, a * a_prev, a)
    a_ref[...] = a
    b_ref[...] = b

    def body(gidx, carry):
        r0 = pl.multiple_of(gidx * SUBLANES, SUBLANES)
        hg = a_ref[pl.ds(r0, SUBLANES), :] * carry + b_ref[pl.ds(r0, SUBLANES), :]
        h_ref[pl.ds(r0, SUBLANES), :] = hg
        return jnp.broadcast_to(hg[SUBLANES - 1:SUBLANES, :], (SUBLANES, bw))

    carry_ref[...] = lax.fori_loop(0, ts // SUBLANES, body, carry_ref[...], unroll=4)

    gate = gate_ref[...].astype(_F32)
    o_ref[...] = (h_ref[...] * (gate * jax.nn.sigmoid(gate))).astype(o_ref.dtype)


def _rglru(proj3, conv_w, conv_b, gate_a_w, gate_a_b, gate_x_w, gate_x_b, lam):
    B, S, _ = proj3.shape
    cw_taps, W = conv_w.shape
    nblk, bw, _ = gate_a_w.shape
    ts = _pick(S, 512)
    row = lambda v: v.reshape(1, W)
    vec_spec = pl.BlockSpec((1, bw), lambda b, n, t: (0, n))
    mat_spec = pl.BlockSpec((None, bw, bw), lambda b, n, t: (n, 0, 0))
    return pl.pallas_call(
        functools.partial(_rglru_kernel, conv_w=cw_taps),
        grid=(B, nblk, S // ts),
        in_specs=[
            pl.BlockSpec((None, ts, bw), lambda b, n, t: (b, t, n)),
            pl.BlockSpec((None, ts, bw), lambda b, n, t: (b, t, nblk + n)),
            pl.BlockSpec((cw_taps, bw), lambda b, n, t: (0, n)),
            vec_spec, mat_spec, vec_spec, mat_spec, vec_spec, vec_spec,
        ],
        out_specs=pl.BlockSpec((None, ts, bw), lambda b, n, t: (b, t, n)),
        out_shape=jax.ShapeDtypeStruct((B, S, W), _BF16),
        scratch_shapes=[
            pltpu.VMEM((ts + SUBLANES, bw), _F32),
            pltpu.VMEM((ts, bw), _F32),
            pltpu.VMEM((ts, bw), _F32),
            pltpu.VMEM((ts, bw), _F32),
            pltpu.VMEM((SUBLANES, bw), _F32),
        ],
        compiler_params=_params(("parallel", "parallel", "arbitrary")),
    )(proj3, proj3, conv_w, row(conv_b), gate_a_w, row(gate_a_b), gate_x_w, row(gate_x_b), row(lam))


def _sb_kernel(q_ref, gate_ref, k_ref, v_ref, m_ref, o_ref, acc_ref, c_ref, *, scale):
    tq, dh = q_ref.shape
    tk = m_ref.shape[0]
    i = pl.program_id(2)
    q = q_ref[...]
    tri = m_ref[...]

    def tile(j, masked):
        ks = pl.multiple_of(j * tk, tk)
        k = k_ref[pl.ds(ks, tk), :]
        v = v_ref[pl.ds(ks, tk), :]
        z = lax.dot_general(q, k, (((1,), (1,)), ((), ())), preferred_element_type=_F32) * scale
        softplus_z = jnp.maximum(z, 0.0) + jnp.log(1.0 + jnp.exp(-jnp.abs(z)))
        log_1m = -softplus_z
        if masked:
            causal = (lax.broadcasted_iota(jnp.int32, (tq, tk), 1)
                      < lax.broadcasted_iota(jnp.int32, (tq, tk), 0))
            log_1m = jnp.where(causal, log_1m, 0.0)
        tail = jnp.dot(log_1m.astype(_BF16), tri, preferred_element_type=_F32) + c_ref[...]
        w = jnp.exp((z - softplus_z) + tail)
        if masked:
            w = jnp.where(causal, w, 0.0)
        acc_ref[...] += jnp.dot(w.astype(_BF16), v, preferred_element_type=_F32)
        c_new = c_ref[...] + jnp.sum(log_1m, axis=-1, keepdims=True)
        c_ref[...] = c_new
        return (jnp.max(c_new) > EXP_F32_ZERO_BELOW).astype(jnp.int32)

    acc_ref[...] = jnp.zeros_like(acc_ref)
    c_ref[...] = jnp.zeros_like(c_ref)
    live = tile(i, True)

    def cond(state):
        j, live = state
        return jnp.logical_and(j >= 0, live > 0)

    def body(state):
        j, _ = state
        return j - 1, tile(j, False)

    lax.while_loop(cond, body, (i - 1, live))

    gate = gate_ref[...].astype(_F32)
    o_ref[...] = (acc_ref[...] * (gate * jax.nn.sigmoid(gate))).astype(o_ref.dtype)


def _sb_attention(proj3, kv3, W):
    B, S, _ = proj3.shape
    dh = SB_HEAD_DIM
    H = W // dh
    tq = _pick(S, MXU_DIM)
    tri = (lax.broadcasted_iota(jnp.int32, (tq, tq), 0)
           > lax.broadcasted_iota(jnp.int32, (tq, tq), 1)).astype(_BF16)
    return pl.pallas_call(
        functools.partial(_sb_kernel, scale=1.0 / math.sqrt(dh)),
        grid=(B, H, S // tq),
        in_specs=[
            pl.BlockSpec((None, tq, dh), lambda b, h, i: (b, i, h)),
            pl.BlockSpec((None, tq, dh), lambda b, h, i: (b, i, H + h)),
            pl.BlockSpec((None, S, dh), lambda b, h, i: (b, 0, h)),
            pl.BlockSpec((None, S, dh), lambda b, h, i: (b, 0, H + h)),
            pl.BlockSpec((tq, tq), lambda b, h, i: (0, 0)),
        ],
        out_specs=pl.BlockSpec((None, tq, dh), lambda b, h, i: (b, i, h)),
        out_shape=jax.ShapeDtypeStruct((B, S, W), _BF16),
        scratch_shapes=[pltpu.VMEM((tq, dh), _F32), pltpu.VMEM((tq, 1), _F32)],
        compiler_params=_params(("parallel", "parallel", "arbitrary")),
    )(proj3, proj3, kv3, kv3, tri)


def _mem_attn_kernel(q_ref, gate_ref, mk_ref, mv_ref, o_ref, *, scale):
    s = lax.dot_general(q_ref[...], mk_ref[...], (((1,), (1,)), ((), ())),
                        preferred_element_type=_F32) * scale
    e = jnp.exp(s - jnp.max(s, axis=-1, keepdims=True))
    denom = jnp.sum(e, axis=-1, keepdims=True)
    p = (e / denom).astype(_BF16)
    o = jnp.dot(p, mv_ref[...], preferred_element_type=_F32)
    gate = gate_ref[...].astype(_F32)
    o_ref[...] = (o * (gate * jax.nn.sigmoid(gate))).astype(o_ref.dtype)


def _mem_attention(proj3, memkv3, W, mem_w):
    B, S, _ = proj3.shape
    M = memkv3.shape[1]
    dh = mem_w // MEM_HEADS
    q0 = (2 * W) // dh
    g0 = (2 * W + mem_w) // dh
    tq = _pick(S, 1024)
    return pl.pallas_call(
        functools.partial(_mem_attn_kernel, scale=1.0 / math.sqrt(dh)),
        grid=(B, S // tq, MEM_HEADS),
        in_specs=[
            pl.BlockSpec((None, tq, dh), lambda b, i, h: (b, i, q0 + h)),
            pl.BlockSpec((None, tq, dh), lambda b, i, h: (b, i, g0 + h)),
            pl.BlockSpec((None, M, dh), lambda b, i, h: (b, 0, h)),
            pl.BlockSpec((None, M, dh), lambda b, i, h: (b, 0, MEM_HEADS + h)),
        ],
        out_specs=pl.BlockSpec((None, tq, dh), lambda b, i, h: (b, i, h)),
        out_shape=jax.ShapeDtypeStruct((B, S, mem_w), _BF16),
        compiler_params=_params(("parallel", "parallel", "parallel")),
    )(proj3, proj3, memkv3, memkv3)


def kernel(x, mem, pre_gain, post_gain, w_in, w_out, w_mem_kv, mem_norm_gain, conv_w, conv_b,
           gate_a_w, gate_a_b, gate_x_w, gate_x_b, lru_lambda, kv_norm_gain, w_kv):
    B, S, D = x.shape
    M = mem.shape[1]
    depth, _, in_cols = w_in.shape
    n_a = conv_w.shape[0]
    W = conv_w.shape[-1]
    mem_w = (in_cols - 2 * W) // 2
    T = B * S
    assert W % SB_HEAD_DIM == 0 and (2 * W) % (mem_w // MEM_HEADS) == 0
    assert w_out.shape[1] == W + mem_w and w_kv.shape[1] == 2 * W

    w_in_b = w_in.astype(_BF16)
    w_out_b = w_out.astype(_BF16)
    w_mem_kv_b = w_mem_kv.astype(_BF16)
    w_kv_b = w_kv.astype(_BF16)
    gate_a_w_b = gate_a_w.astype(_BF16)
    gate_x_w_b = gate_x_w.astype(_BF16)

    mem2 = mem.reshape(B * M, D)
    h = x.reshape(T, D)
    kv3 = None
    for layer in range(depth):
        proj3 = _norm_matmul(h, pre_gain[layer], w_in_b[layer]).reshape(B, S, in_cols)
        if layer < n_a:
            br_out = _rglru(proj3, conv_w[layer], conv_b[layer], gate_a_w_b[layer], gate_a_b[layer],
                            gate_x_w_b[layer], gate_x_b[layer], lru_lambda[layer])
        else:
            br_out = _sb_attention(proj3, kv3, W)
        memkv3 = _norm_matmul(mem2, mem_norm_gain, w_mem_kv_b[layer]).reshape(B, M, 2 * mem_w)
        m_out = _mem_attention(proj3, memkv3, W, mem_w)
        h = _out_proj(br_out.reshape(T, W), m_out.reshape(T, mem_w), w_out_b[layer], h,
                      post_gain[layer])
        if layer == n_a - 1:
            kv3 = _norm_matmul(h, kv_norm_gain, w_kv_b).reshape(B, S, 2 * W)
    return h.reshape(B, S, D)
```
